```python
import jax, jax.numpy as jnp
from jax import lax
import numpy as np

D_MODEL = 1024
BATCH = 8
SEQ = 2048
DEPTH = 1

CHUNK = 64
N_META = 16
ROPE_THETA = 10000.0
RMS_EPS = 1e-6
D_MIX = D_MODEL
HEAD_DIM = 64
D_ATTN = D_MIX // 2
ATTN_HEADS = D_ATTN // HEAD_DIM
IDX_HEADS = 8
IDX_DIM = 64
TOPK_MAX = 256
Q_BLOCK = 128
POOL_WINDOWS = (2, 4, 8, 16)
POOL_GROUPS = 4
D_POOL = D_MIX - D_ATTN
POOL_CH = D_POOL // POOL_GROUPS
OFF_Q = 0
OFF_K = OFF_Q + D_ATTN
OFF_V = OFF_K + D_ATTN
OFF_POOL = OFF_V + D_ATTN
OFF_IQ = OFF_POOL + D_POOL
OFF_IK = OFF_IQ + IDX_HEADS * IDX_DIM
OFF_IW = OFF_IK + IDX_DIM
D_IN_PROJ = OFF_IW + IDX_HEADS
PEER_HEADS = 8
N_KEYS = 128
N_EXPERTS = N_KEYS * N_KEYS
PEER_QDIM = 256
PEER_HALF = PEER_QDIM // 2
PEER_TOPK = 16
TOK_BLOCK = 256

kernel_name = 'hymba_dsa_pool_peer_layer'


def rms_norm(x, g):
    xf = x.astype(jnp.float32)
    y = xf * lax.rsqrt(jnp.mean(xf * xf, axis=-1, keepdims=True) + RMS_EPS)
    return (y * g.astype(jnp.float32)).astype(x.dtype)


def rope_tables(T, dim):
    inv = ROPE_THETA ** (-jnp.arange(0, dim, 2, dtype=jnp.float32) / dim)
    ang = jnp.arange(T, dtype=jnp.float32)[:, None] * inv[None, :]
    return jnp.cos(ang), jnp.sin(ang)


def apply_rope(x, cos, sin):
    xf = x.astype(jnp.float32)
    x1, x2 = jnp.split(xf, 2, axis=-1)
    c = cos[None, :, None, :]
    s = sin[None, :, None, :]
    return jnp.concatenate([x1 * c - x2 * s, x2 * c + x1 * s], axis=-1).astype(x.dtype)


def chunk_ids(T):
    pos = jnp.arange(T, dtype=jnp.int32)
    return jnp.where(pos < N_META, 0, 1 + (pos - N_META) // CHUNK)


def dsa_attention(q, k, v, qi, ki, wi, k_sel):
    B, T = q.shape[0], q.shape[1]
    nb = -(-T // Q_BLOCK)
    Tp = nb * Q_BLOCK
    pad = Tp - T
    cid = chunk_ids(T)
    cq_all = chunk_ids(Tp).reshape(nb, Q_BLOCK)

    def blocks(a):
        a = jnp.pad(a, [(0, 0), (0, pad)] + [(0, 0)] * (a.ndim - 2))
        return a.reshape((B, nb, Q_BLOCK) + a.shape[2:]).swapaxes(0, 1)

    gather = jax.vmap(lambda a, i: a[i])

    def one_block(args):
        qb, qib, wb, cqb = args
        rel = jax.nn.relu(jnp.einsum('bqhd,bsd->bqhs', qib, ki, preferred_element_type=jnp.float32))
        score = jnp.einsum('bqhs,bqh->bqs', rel, wb.astype(jnp.float32))
        vis = cid[None, :] <= cqb[:, None]
        score = jnp.where(vis[None], score, -jnp.inf)
        _, idx = lax.top_k(score, k_sel)
        ks = gather(k, idx)
        vs = gather(v, idx)
        ok = cid[idx] <= cqb[None, :, None]
        logits = jnp.einsum('bqhd,bqkhd->bhqk', qb, ks,
                            preferred_element_type=jnp.float32) * (HEAD_DIM ** -0.5)
        logits = jnp.where(ok[:, None], logits, -jnp.inf)
        p = jax.nn.softmax(logits, axis=-1).astype(vs.dtype)
        return jnp.einsum('bhqk,bqkhd->bqhd', p, vs)

    out = lax.map(one_block, (blocks(q), blocks(qi), blocks(wi), cq_all))
    out = out.swapaxes(0, 1).reshape(B, Tp, q.shape[2], q.shape[3])[:, :T]
    return out


def pool_mixer(u, pool_w, pool_scale):
    B, T, _ = u.shape
    uf = u.astype(jnp.float32).reshape(B, T, POOL_GROUPS, POOL_CH)
    c = jnp.cumsum(jnp.pad(uf, ((0, 0), (1, 0), (0, 0), (0, 0))), axis=1)
    t = jnp.arange(T, dtype=jnp.int32)
    means = []
    for g, w in enumerate(POOL_WINDOWS):
        lo = jnp.maximum(t + 1 - w, 0)
        cnt = jnp.minimum(t + 1, w).astype(jnp.float32)
        means.append((c[:, 1:, g] - c[:, lo, g]) / cnt[None, :, None])
    y = (jnp.stack(means, axis=2) - uf).astype(u.dtype)
    y = jnp.einsum('btgc,gcd->btgd', y, pool_w).reshape(B, T, D_POOL)
    return y * pool_scale


def peer_ffn(h, peer_wq, sub_keys1, sub_keys2, peer_u, peer_v):
    B, T, D = h.shape
    q = (h @ peer_wq).reshape(B, T, PEER_HEADS, PEER_QDIM)
    q1, q2 = q[..., :PEER_HALF], q[..., PEER_HALF:]
    s1 = jnp.einsum('bthd,nd->bthn', q1, sub_keys1, preferred_element_type=jnp.float32)
    s2 = jnp.einsum('bthd,nd->bthn', q2, sub_keys2, preferred_element_type=jnp.float32)
    v1, i1 = lax.top_k(s1, PEER_TOPK)
    v2, i2 = lax.top_k(s2, PEER_TOPK)
    n_cand = PEER_TOPK * PEER_TOPK
    cand_s = (v1[..., :, None] + v2[..., None, :]).reshape(B, T, PEER_HEADS, n_cand)
    cand_i = (i1[..., :, None] * N_KEYS + i2[..., None, :]).reshape(B, T, PEER_HEADS, n_cand)
    top_s, pos = lax.top_k(cand_s, PEER_TOPK)
    eid = jnp.take_along_axis(cand_i, pos, axis=-1)
    gate = jax.nn.softmax(top_s, axis=-1)
    n = B * T
    npad = -(-n // TOK_BLOCK) * TOK_BLOCK
    kk = PEER_HEADS * PEER_TOPK
    hf = jnp.pad(h.reshape(n, D), ((0, npad - n), (0, 0))).reshape(-1, TOK_BLOCK, D)
    ef = jnp.pad(eid.reshape(n, kk), ((0, npad - n), (0, 0))).reshape(-1, TOK_BLOCK, kk)
    gf = jnp.pad(gate.reshape(n, kk), ((0, npad - n), (0, 0))).reshape(-1, TOK_BLOCK, kk)

    def one_block(args):
        hb, eb, gb = args
        a = jnp.einsum('td,tkd->tk', hb, peer_u[eb], preferred_element_type=jnp.float32)
        a = jax.nn.gelu(a, approximate=False)
        return jnp.einsum('tk,tkd->td', (gb * a).astype(hb.dtype), peer_v[eb])

    out = lax.map(one_block, (hf, ef, gf))
    return out.reshape(npad, D)[:n].reshape(B, T, D)


def setup_inputs(seed: int = 0) -> dict:
    key = jax.random.key(seed)
    ks = jax.random.split(key, 16)
    nrm = jax.random.normal
    f32 = jnp.float32
    return {
        'x': nrm(ks[0], (BATCH, SEQ, D_MODEL), f32),
        'meta_tokens': nrm(ks[1], (N_META, D_MODEL), f32),
        'norm1_g': 1.0 + 0.02 * nrm(ks[2], (DEPTH, D_MODEL), f32),
        'w_in': nrm(ks[3], (DEPTH, D_MODEL, D_IN_PROJ), f32) * D_MODEL ** -0.5,
        'q_norm_g': 1.0 + 0.02 * nrm(ks[4], (DEPTH, HEAD_DIM), f32),
        'k_norm_g': 1.0 + 0.02 * nrm(ks[5], (DEPTH, HEAD_DIM), f32),
        'pool_w': nrm(ks[6], (DEPTH, POOL_GROUPS, POOL_CH, POOL_CH), f32) * POOL_CH ** -0.5,
        'pool_scale': 1.0 + 0.1 * nrm(ks[7], (DEPTH, D_POOL), f32),
        'w_out': nrm(ks[8], (DEPTH, D_MIX, D_MODEL), f32) * D_MIX ** -0.5,
        'norm2_g': 1.0 + 0.02 * nrm(ks[9], (DEPTH, D_MODEL), f32),
        'peer_wq': nrm(ks[10], (DEPTH, D_MODEL, PEER_HEADS * PEER_QDIM), f32) * D_MODEL ** -0.5,
        'sub_keys1': nrm(ks[11], (DEPTH, N_KEYS, PEER_HALF), f32) * PEER_HALF ** -0.5,
        'sub_keys2': nrm(ks[12], (DEPTH, N_KEYS, PEER_HALF), f32) * PEER_HALF ** -0.5,
        'peer_u': nrm(ks[13], (DEPTH, N_EXPERTS, D_MODEL), f32) * D_MODEL ** -0.5,
        'peer_v': nrm(ks[14], (DEPTH, N_EXPERTS, D_MODEL), f32) * PEER_HEADS ** -0.5,
    }


def reference(x, meta_tokens, norm1_g, w_in, q_norm_g, k_norm_g, pool_w, pool_scale,
              w_out, norm2_g, peer_wq, sub_keys1, sub_keys2, peer_u, peer_v):
    B, S, D = x.shape
    k_sel = min(TOPK_MAX, S // 4)
    T = S + N_META
    h = jnp.concatenate([jnp.broadcast_to(meta_tokens.astype(x.dtype)[None], (B, N_META, D)), x], axis=1)
    cos, sin = rope_tables(T, HEAD_DIM)
    for l in range(DEPTH):
        z = rms_norm(h, norm1_g[l]) @ w_in[l]
        q = z[..., OFF_Q:OFF_K].reshape(B, T, ATTN_HEADS, HEAD_DIM)
        k = z[..., OFF_K:OFF_V].reshape(B, T, ATTN_HEADS, HEAD_DIM)
        v = z[..., OFF_V:OFF_POOL].reshape(B, T, ATTN_HEADS, HEAD_DIM)
        q = apply_rope(rms_norm(q, q_norm_g[l]), cos, sin)
        k = apply_rope(rms_norm(k, k_norm_g[l]), cos, sin)
        qi = apply_rope(z[..., OFF_IQ:OFF_IK].reshape(B, T, IDX_HEADS, IDX_DIM), cos, sin)
        ki = apply_rope(z[..., OFF_IK:OFF_IW][:, :, None, :], cos, sin)[:, :, 0, :]
        wi = z[..., OFF_IW:D_IN_PROJ] * (IDX_HEADS ** -0.5 * IDX_DIM ** -0.5)
        a = dsa_attention(q, k, v, qi, ki, wi, k_sel).reshape(B, T, D_ATTN)
        p = pool_mixer(z[..., OFF_POOL:OFF_IQ], pool_w[l], pool_scale[l])
        h = h + jnp.concatenate([a, p], axis=-1) @ w_out[l]
        h = h + peer_ffn(rms_norm(h, norm2_g[l]), peer_wq[l], sub_keys1[l], sub_keys2[l],
                         peer_u[l], peer_v[l])
    return h[:, N_META:]
```

```python
import functools

import jax
import jax.numpy as jnp
import numpy as np
from jax import lax
from jax.experimental import pallas as pl
from jax.experimental.pallas import tpu as pltpu

F32 = jnp.float32
BF16 = jnp.bfloat16
I32 = jnp.int32

D_MODEL = 1024
N_META = 16
CHUNK = 64
ROPE_THETA = 10000.0
RMS_EPS = 1e-6
HEAD_DIM = 64
ATTN_HEADS = 8
HEAD_PAIRS = ATTN_HEADS // 2
D_ATTN = 512
IDX_HEADS = 8
TOPK = 256
POOL_WINDOWS = (2, 4, 8, 16)
POOL_CH = 128
D_POOL = 512
PEER_HEADS = 8
N_KEYS = 128
PEER_HALF = 128
PEER_TOPK = 16

LANES = 128
META_ROWS = 128
ZT_ROWS = 2128
OFF_QT, OFF_KT, OFF_VT, OFF_IQT, OFF_IKT, OFF_IWT = 0, 512, 1024, 1536, 2048, 2112
INT_MIN = -2 ** 31
VMEM_LIMIT = 56 * 1024 * 1024


def _nt(a, b):
    return lax.dot_general(a, b, (((1,), (1,)), ((), ())), preferred_element_type=F32)


def _tn(a, b):
    return lax.dot_general(a, b, (((0,), (0,)), ((), ())), preferred_element_type=F32)


def _mm(a, b):
    return jnp.dot(a, b, preferred_element_type=F32)


def _rms_rows(x, g):
    ms = jnp.mean(x * x, axis=-1, keepdims=True)
    return x * lax.rsqrt(ms + RMS_EPS) * g


def _project_transposed(hn, wt_ref, gq, gk, cos, sin):
    zt = _nt(wt_ref[...], hn)

    def rope(xh):
        x1, x2 = xh[:32], xh[32:]
        return jnp.concatenate([x1 * cos - x2 * sin, x2 * cos + x1 * sin], axis=0)

    def headnorm(xh, g):
        ms = jnp.mean(xh * xh, axis=0, keepdims=True)
        return xh * lax.rsqrt(ms + RMS_EPS) * g

    def head(off, h):
        return zt[off + HEAD_DIM * h: off + HEAD_DIM * (h + 1)]

    q_pairs, k_pairs, v_pairs, qi_pairs = [], [], [], []
    for j in range(HEAD_PAIRS):
        qs = [rope(headnorm(head(OFF_QT, 2 * j + s), gq)) * (HEAD_DIM ** -0.5) for s in range(2)]
        ks = [rope(headnorm(head(OFF_KT, 2 * j + s), gk)) for s in range(2)]
        qis = [rope(head(OFF_IQT, 2 * j + s)) for s in range(2)]
        q_pairs.append(jnp.concatenate(qs, axis=0).T.astype(BF16))
        k_pairs.append(jnp.concatenate(ks, axis=0).T.astype(BF16))
        qi_pairs.append(jnp.concatenate(qis, axis=0).T.astype(BF16))
        v_pairs.append(zt[OFF_VT + LANES * j: OFF_VT + LANES * (j + 1)].T.astype(BF16))
    kit = rope(zt[OFF_IKT: OFF_IKT + HEAD_DIM])
    ki2 = jnp.concatenate([kit, kit], axis=0).T.astype(BF16)
    wi = zt[OFF_IWT: OFF_IWT + IDX_HEADS] * (IDX_HEADS ** -0.5 * HEAD_DIM ** -0.5)
    return q_pairs, k_pairs, v_pairs, qi_pairs, ki2, wi


def _meta_body(x_ref, g1_ref, wt_ref, wpool_ref, gq_ref, gk_ref, cos_ref, sin_ref,
               k_ref, v_ref, ki_ref, u_ref):
    hn = _rms_rows(x_ref[...], g1_ref[...]).astype(BF16)
    _, k_pairs, v_pairs, _, ki2, _ = _project_transposed(
        hn, wt_ref, gq_ref[...], gk_ref[...], cos_ref[...], sin_ref[...])
    for j in range(HEAD_PAIRS):
        k_ref[j] = k_pairs[j]
        v_ref[j] = v_pairs[j]
    ki_ref[...] = ki2
    u_ref[...] = _mm(hn, wpool_ref[...])


def _proj_body(x_ref, g1_ref, wt_ref, wpool_ref, gq_ref, gk_ref, cos_ref, sin_ref,
               poolw_ref, pscale_ref, umeta_ref,
               q_ref, k_ref, v_ref, qi_ref, ki_ref, wi_ref, p_ref, ext_ref, *, tile, tiles_per_seq):
    i = pl.program_id(0)
    hn = _rms_rows(x_ref[...], g1_ref[...]).astype(BF16)
    q_pairs, k_pairs, v_pairs, qi_pairs, ki2, wi = _project_transposed(
        hn, wt_ref, gq_ref[...], gk_ref[...], cos_ref[...], sin_ref[...])
    for j in range(HEAD_PAIRS):
        q_ref[j] = q_pairs[j]
        k_ref[j] = k_pairs[j]
        v_ref[j] = v_pairs[j]
        qi_ref[j] = qi_pairs[j]
    ki_ref[...] = ki2
    wi_ref[...] = wi

    u = _mm(hn, wpool_ref[...])

    @pl.when(i % tiles_per_seq == 0)
    def _():
        ext_ref[0:N_META, :] = umeta_ref[0:N_META, :]

    ext_ref[N_META:N_META + tile, :] = u
    for g, w in enumerate(POOL_WINDOWS):
        cols = slice(POOL_CH * g, POOL_CH * (g + 1))
        acc = ext_ref[N_META:N_META + tile, cols]
        for s in range(1, w):
            acc = acc + ext_ref[N_META - s:N_META - s + tile, cols]
        y = (acc * (1.0 / w) - u[:, cols]).astype(BF16)
        pg = _mm(y, poolw_ref[g]) * pscale_ref[:, cols]
        p_ref[:, cols] = pg.astype(BF16)
    ext_ref[0:N_META, :] = u[tile - N_META:tile, :]


def _sortable(x):
    b = pltpu.bitcast(x, I32)
    k = b ^ ((b >> 31) & 0x7FFFFFFF)
    return jnp.where(k == -1, 0, k)


def _attn_body(q_ref, qi_ref, wi_ref, k_ref, v_ref, ki_ref, km_ref, vm_ref, kim_ref,
               a_ref, key_ref, idx_ref, bias_ref, *, n_real):
    i = pl.program_id(1)
    n_rows = META_ROWS + n_real
    lane = lax.broadcasted_iota(I32, (LANES, LANES), 1)
    lo = lane < HEAD_DIM

    def half_mask(pair, h):
        keep = lo if h % 2 == 0 else jnp.logical_not(lo)
        return jnp.where(keep, pair, jnp.zeros_like(pair))

    kim = kim_ref[...]
    kir = ki_ref[...]
    sc_m = jnp.zeros((META_ROWS, LANES), F32)
    sc_r = jnp.zeros((n_real, LANES), F32)
    for h in range(IDX_HEADS):
        qih = half_mask(qi_ref[h // 2], h)
        w = wi_ref[h:h + 1, :]
        sc_m = sc_m + w * jnp.maximum(_nt(kim, qih), 0.0)
        sc_r = sc_r + w * jnp.maximum(_nt(kir, qih), 0.0)

    row_m = lax.broadcasted_iota(I32, (META_ROWS, LANES), 0)
    row_r = lax.broadcasted_iota(I32, (n_real, LANES), 0)
    qchunk = 2 * i + (lax.broadcasted_iota(I32, (n_real, LANES), 1) >> 6)
    key_ref[0:META_ROWS, :] = jnp.where(row_m < N_META, _sortable(sc_m), INT_MIN)
    key_ref[META_ROWS:n_rows, :] = jnp.where((row_r >> 6) <= qchunk, _sortable(sc_r), INT_MIN)

    def count(pred_fn):
        return jnp.sum(pred_fn().astype(I32), axis=0, keepdims=True)

    c0 = count(lambda: key_ref[...] >= 0)
    prefix0 = jnp.where(c0 >= TOPK, 0, INT_MIN).astype(I32)

    def search(it, prefix):
        cand = prefix + (jnp.int32(1) << (30 - it))
        c = count(lambda: key_ref[...] >= cand)
        return jnp.where(c >= TOPK, cand, prefix)

    thr = lax.fori_loop(0, 31, search, prefix0)

    need = TOPK - count(lambda: key_ref[...] > thr)
    thr_eq = jnp.maximum(thr, INT_MIN + 1)
    rows = lax.broadcasted_iota(I32, (n_rows, LANES), 0)
    big = jnp.int32(1 << 20)
    idx_ref[...] = jnp.where(key_ref[...] == thr_eq, rows, big)

    def tie_search(it, p):
        cand = p + (jnp.int32(1) << (11 - it))
        f = count(lambda: idx_ref[...] < cand)
        return jnp.where(f < need, cand, p)

    last_eq = lax.fori_loop(0, 12, tie_search, jnp.zeros((1, LANES), I32))
    neg = jnp.float32(-jnp.inf)
    bias_ref[...] = jnp.where(key_ref[...] > thr, 0.0, jnp.where(idx_ref[...] <= last_eq, 0.0, neg))

    for j in range(HEAD_PAIRS):
        outs = []
        for s in range(2):
            qh = half_mask(q_ref[j], s)
            lg_m = _nt(km_ref[j], qh) + bias_ref[0:META_ROWS, :]
            lg_r = _nt(k_ref[j], qh) + bias_ref[META_ROWS:n_rows, :]
            mx = jnp.maximum(jnp.max(lg_m, axis=0, keepdims=True), jnp.max(lg_r, axis=0, keepdims=True))
            e_m = jnp.exp(lg_m - mx)
            e_r = jnp.exp(lg_r - mx)
            inv = 1.0 / (jnp.sum(e_m, axis=0, keepdims=True) + jnp.sum(e_r, axis=0, keepdims=True))
            o = _tn((e_m * inv).astype(BF16), vm_ref[j]) + _tn((e_r * inv).astype(BF16), v_ref[j])
            outs.append(o)
        a_ref[:, LANES * j:LANES * (j + 1)] = jnp.where(lo, outs[0], outs[1]).astype(BF16)


def _top16_rows(s):
    neg = jnp.float32(-jnp.inf)
    vals = []
    for _ in range(PEER_TOPK):
        m = jnp.max(s, axis=0, keepdims=True)
        vals.append(m)
        s = jnp.where(s == m, neg, s)
    return jnp.concatenate(vals, axis=0)


def _post_body(x_ref, a_ref, p_ref, wout_ref, g2_ref, wq_ref, sk1_ref, sk2_ref,
               h2_ref, hn_ref, s1_ref, s2_ref, e1_ref, e2_ref, thr_ref, *, tile):
    ap = jnp.concatenate([a_ref[...], p_ref[...]], axis=1)
    h2 = x_ref[...] + _mm(ap, wout_ref[...])
    h2_ref[...] = h2
    hn = _rms_rows(h2, g2_ref[...]).astype(BF16)
    hn_ref[...] = hn
    q = _mm(hn, wq_ref[...]).astype(BF16)
    neg = jnp.float32(-jnp.inf)
    thrs = []
    for h in range(PEER_HEADS):
        base = 2 * PEER_HALF * h
        s1 = _nt(sk1_ref[...], q[:, base:base + PEER_HALF])
        s2 = _nt(sk2_ref[...], q[:, base + PEER_HALF:base + 2 * PEER_HALF])
        v1 = _top16_rows(s1)
        v2 = _top16_rows(s2)
        blocks = [v1[0:1] + v2]
        for a in range(1, 8):
            nb = PEER_TOPK // (a + 1)
            rowid = lax.broadcasted_iota(I32, (8, tile), 0)
            blocks.append(jnp.where(rowid < nb, v1[a:a + 1] + v2[0:8], neg))
        blocks.append(v1[8:16] + v2[0:1])
        top = _top16_rows(jnp.concatenate(blocks, axis=0))
        thr = top[PEER_TOPK - 1:PEER_TOPK]
        z = jnp.sum(jnp.exp(top - top[0:1]), axis=0, keepdims=True)
        s1_ref[h] = s1
        s2_ref[h] = s2
        e1_ref[h] = jnp.exp(s1 - v1[0:1]) * (1.0 / z)
        e2_ref[h] = jnp.exp(s2 - v2[0:1])
        thrs.append(thr)
    thr_ref[...] = jnp.concatenate(thrs, axis=0)


def _gelu(x):
    return 0.5 * x * (1.0 + lax.erf(x * np.float32(2.0 ** -0.5)))


def _peer_body(hn_ref, u_ref, v_ref, s1_ref, e1_ref, s2_ref, e2_ref, thr_ref, h2_ref,
               o_ref, acc_ref, *, e1_per_tile):
    e = pl.program_id(1)

    @pl.when(e == 0)
    def _():
        acc_ref[...] = jnp.zeros_like(acc_ref)

    at = _nt(u_ref[...], hn_ref[...])
    parts = []
    for r in range(e1_per_tile):
        g = None
        for h in range(PEER_HEADS):
            s = s1_ref[h, r:r + 1, :] + s2_ref[h]
            val = e1_ref[h, r:r + 1, :] * e2_ref[h]
            gh = jnp.where(s >= thr_ref[h:h + 1, :], val, 0.0)
            g = gh if g is None else g + gh
        parts.append((g * _gelu(at[N_KEYS * r:N_KEYS * (r + 1)])).astype(BF16))
    pt = jnp.concatenate(parts, axis=0)
    acc_ref[...] += _tn(pt, v_ref[...])

    @pl.when(e == pl.num_programs(1) - 1)
    def _():
        o_ref[...] = h2_ref[...] + acc_ref[...]


def _rope_tables_t(n_pos):
    inv = ROPE_THETA ** (-jnp.arange(0, HEAD_DIM, 2, dtype=F32) / HEAD_DIM)
    ang = jnp.arange(n_pos, dtype=F32)[:, None] * inv[None, :]
    return jnp.cos(ang).T, jnp.sin(ang).T


def _full(shape):
    return pl.BlockSpec(shape, lambda *_: (0,) * len(shape))


def _params(sem):
    return pltpu.CompilerParams(dimension_semantics=sem, vmem_limit_bytes=VMEM_LIMIT)


def _prepare(meta_tokens, norm1_g, w_in, q_norm_g, k_norm_g, pool_w, pool_scale, w_out, norm2_g,
             peer_wq, sub_keys1, sub_keys2, peer_u, peer_v, seq):
    l = 0
    w = w_in[l]
    wt = jnp.concatenate([w[:, 0:1536], w[:, 2048:2632]], axis=1).T
    wt = jnp.pad(wt, ((0, ZT_ROWS - wt.shape[0]), (0, 0))).astype(BF16)
    cos_t, sin_t = _rope_tables_t(N_META + seq)
    return dict(
        wt=wt,
        wpool=w[:, 1536:2048].astype(BF16),
        g1=norm1_g[l].reshape(1, D_MODEL),
        g2=norm2_g[l].reshape(1, D_MODEL),
        gq=q_norm_g[l].reshape(HEAD_DIM, 1),
        gk=k_norm_g[l].reshape(HEAD_DIM, 1),
        poolw=pool_w[l].astype(BF16),
        pscale=pool_scale[l].reshape(1, D_POOL),
        wout=w_out[l].astype(BF16),
        wq=peer_wq[l].astype(BF16),
        sk1=sub_keys1[l].astype(BF16),
        sk2=sub_keys2[l].astype(BF16),
        ub=peer_u[l].astype(BF16),
        vb=peer_v[l].astype(BF16),
        cos_m=cos_t[:, :META_ROWS], sin_m=sin_t[:, :META_ROWS],
        cos_r=cos_t[:, N_META:], sin_r=sin_t[:, N_META:],
        meta_pad=jnp.pad(meta_tokens.astype(F32), ((0, META_ROWS - N_META), (0, 0))),
    )


def _run_meta_proj(pp):
    D = D_MODEL
    return pl.pallas_call(
        _meta_body,
        grid=(1,),
        in_specs=[_full((META_ROWS, D)), _full((1, D)), _full((ZT_ROWS, D)), _full((D, D_POOL)),
                  _full((HEAD_DIM, 1)), _full((HEAD_DIM, 1)), _full((32, META_ROWS)), _full((32, META_ROWS))],
        out_specs=[_full((HEAD_PAIRS, META_ROWS, LANES)), _full((HEAD_PAIRS, META_ROWS, LANES)),
                   _full((META_ROWS, LANES)), _full((META_ROWS, D_POOL))],
        out_shape=[jax.ShapeDtypeStruct((HEAD_PAIRS, META_ROWS, LANES), BF16),
                   jax.ShapeDtypeStruct((HEAD_PAIRS, META_ROWS, LANES), BF16),
                   jax.ShapeDtypeStruct((META_ROWS, LANES), BF16),
                   jax.ShapeDtypeStruct((META_ROWS, D_POOL), F32)],
        compiler_params=_params(("arbitrary",)),
        name="meta_proj",
    )(pp["meta_pad"], pp["g1"], pp["wt"], pp["wpool"], pp["gq"], pp["gk"], pp["cos_m"], pp["sin_m"])


def _run_proj(x2, pp, umeta, seq):
    N, D = x2.shape
    t1 = 512
    tps = seq // t1
    pair_spec = pl.BlockSpec((HEAD_PAIRS, t1, LANES), lambda i: (0, i, 0))
    pair_shape = jax.ShapeDtypeStruct((HEAD_PAIRS, N, LANES), BF16)
    return pl.pallas_call(
        functools.partial(_proj_body, tile=t1, tiles_per_seq=tps),
        grid=(N // t1,),
        in_specs=[pl.BlockSpec((t1, D), lambda i: (i, 0)), _full((1, D)), _full((ZT_ROWS, D)),
                  _full((D, D_POOL)), _full((HEAD_DIM, 1)), _full((HEAD_DIM, 1)),
                  pl.BlockSpec((32, t1), lambda i: (0, i % tps)), pl.BlockSpec((32, t1), lambda i: (0, i % tps)),
                  _full((4, POOL_CH, POOL_CH)), _full((1, D_POOL)), _full((META_ROWS, D_POOL))],
        out_specs=[pair_spec, pair_spec, pair_spec, pair_spec,
                   pl.BlockSpec((t1, LANES), lambda i: (i, 0)),
                   pl.BlockSpec((IDX_HEADS, t1), lambda i: (0, i)),
                   pl.BlockSpec((t1, D_POOL), lambda i: (i, 0))],
        out_shape=[pair_shape, pair_shape, pair_shape, pair_shape,
                   jax.ShapeDtypeStruct((N, LANES), BF16),
                   jax.ShapeDtypeStruct((IDX_HEADS, N), F32),
                   jax.ShapeDtypeStruct((N, D_POOL), BF16)],
        scratch_shapes=[pltpu.VMEM((N_META + t1, D_POOL), F32)],
        compiler_params=_params(("arbitrary",)),
        name="proj",
    )(x2, pp["g1"], pp["wt"], pp["wpool"], pp["gq"], pp["gk"], pp["cos_r"], pp["sin_r"],
      pp["poolw"], pp["pscale"], umeta)


def _run_attention(q, qi, wi, k, v, ki, km, vm, kim, batch, seq):
    N = batch * seq
    nqb = seq // LANES
    qblk = pl.BlockSpec((HEAD_PAIRS, LANES, LANES), lambda b, i: (0, b * nqb + i, 0))
    kblk = pl.BlockSpec((HEAD_PAIRS, seq, LANES), lambda b, i: (0, b, 0))
    n_rows = META_ROWS + seq
    return pl.pallas_call(
        functools.partial(_attn_body, n_real=seq),
        grid=(batch, nqb),
        in_specs=[qblk, qblk, pl.BlockSpec((IDX_HEADS, LANES), lambda b, i: (0, b * nqb + i)),
                  kblk, kblk, pl.BlockSpec((seq, LANES), lambda b, i: (b, 0)),
                  _full((HEAD_PAIRS, META_ROWS, LANES)), _full((HEAD_PAIRS, META_ROWS, LANES)),
                  _full((META_ROWS, LANES))],
        out_specs=pl.BlockSpec((LANES, D_ATTN), lambda b, i: (b * nqb + i, 0)),
        out_shape=jax.ShapeDtypeStruct((N, D_ATTN), BF16),
        scratch_shapes=[pltpu.VMEM((n_rows, LANES), I32), pltpu.VMEM((n_rows, LANES), I32),
                        pltpu.VMEM((n_rows, LANES), F32)],
        compiler_params=_params(("arbitrary", "arbitrary")),
        name="attention",
    )(q, qi, wi, k, v, ki, km, vm, kim)


def _run_post(x2, a, p, pp):
    N, D = x2.shape
    t3 = 256
    stat_spec = pl.BlockSpec((PEER_HEADS, N_KEYS, t3), lambda i: (0, 0, i))
    stat_shape = jax.ShapeDtypeStruct((PEER_HEADS, N_KEYS, N), F32)
    return pl.pallas_call(
        functools.partial(_post_body, tile=t3),
        grid=(N // t3,),
        in_specs=[pl.BlockSpec((t3, D), lambda i: (i, 0)), pl.BlockSpec((t3, D_ATTN), lambda i: (i, 0)),
                  pl.BlockSpec((t3, D_POOL), lambda i: (i, 0)), _full((D, D)), _full((1, D)),
                  _full((D, 2 * PEER_HALF * PEER_HEADS)), _full((N_KEYS, PEER_HALF)), _full((N_KEYS, PEER_HALF))],
        out_specs=[pl.BlockSpec((t3, D), lambda i: (i, 0)), pl.BlockSpec((t3, D), lambda i: (i, 0)),
                   stat_spec, stat_spec, stat_spec, stat_spec,
                   pl.BlockSpec((PEER_HEADS, t3), lambda i: (0, i))],
        out_shape=[jax.ShapeDtypeStruct((N, D), F32), jax.ShapeDtypeStruct((N, D), BF16),
                   stat_shape, stat_shape, stat_shape, stat_shape,
                   jax.ShapeDtypeStruct((PEER_HEADS, N), F32)],
        compiler_params=_params(("arbitrary",)),
        name="post",
    )(x2, a, p, pp["wout"], pp["g2"], pp["wq"], pp["sk1"], pp["sk2"])


def _run_peer(hn2, s1t, e1t, s2t, e2t, thr, h2, pp):
    N, D = h2.shape
    t4 = 256
    e1_per_tile = 8
    et = e1_per_tile * N_KEYS
    n_et = (N_KEYS * N_KEYS) // et
    row_spec = pl.BlockSpec((PEER_HEADS, e1_per_tile, t4), lambda t, e: (0, e, t))
    col_spec = pl.BlockSpec((PEER_HEADS, N_KEYS, t4), lambda t, e: (0, 0, t))
    return pl.pallas_call(
        functools.partial(_peer_body, e1_per_tile=e1_per_tile),
        grid=(N // t4, n_et),
        in_specs=[pl.BlockSpec((t4, D), lambda t, e: (t, 0)),
                  pl.BlockSpec((et, D), lambda t, e: (e, 0)), pl.BlockSpec((et, D), lambda t, e: (e, 0)),
                  row_spec, row_spec, col_spec, col_spec,
                  pl.BlockSpec((PEER_HEADS, t4), lambda t, e: (0, t)),
                  pl.BlockSpec((t4, D), lambda t, e: (t, 0))],
        out_specs=pl.BlockSpec((t4, D), lambda t, e: (t, 0)),
        out_shape=jax.ShapeDtypeStruct((N, D), F32),
        scratch_shapes=[pltpu.VMEM((t4, D), F32)],
        compiler_params=_params(("arbitrary", "arbitrary")),
        name="peer",
    )(hn2, pp["ub"], pp["vb"], s1t, e1t, s2t, e2t, thr, h2)


def kernel(x, meta_tokens, norm1_g, w_in, q_norm_g, k_norm_g, pool_w, pool_scale, w_out, norm2_g,
           peer_wq, sub_keys1, sub_keys2, peer_u, peer_v):
    B, S, D = x.shape
    assert D == D_MODEL and S % 512 == 0 and min(TOPK, S // 4) == TOPK
    x2 = x.reshape(B * S, D)
    pp = _prepare(meta_tokens, norm1_g, w_in, q_norm_g, k_norm_g, pool_w, pool_scale, w_out, norm2_g,
                  peer_wq, sub_keys1, sub_keys2, peer_u, peer_v, S)
    km, vm, kim, umeta = _run_meta_proj(pp)
    q, k, v, qi, ki, wi, p = _run_proj(x2, pp, umeta, S)
    a = _run_attention(q, qi, wi, k, v, ki, km, vm, kim, B, S)
    h2, hn2, s1t, s2t, e1t, e2t, thr = _run_post(x2, a, p, pp)
    out = _run_peer(hn2, s1t, e1t, s2t, e2t, thr, h2, pp)
    return out.reshape(B, S, D)
```

```python
import functools

import jax
import jax.numpy as jnp
import numpy as np
from jax import lax
from jax.experimental import pallas as pl
from jax.experimental.pallas import tpu as pltpu

F32 = jnp.float32
BF16 = jnp.bfloat16
I32 = jnp.int32

D_MODEL = 1024
N_META = 16
CHUNK = 64
ROPE_THETA = 10000.0
RMS_EPS = 1e-6
HEAD_DIM = 64
ATTN_HEADS = 8
HEAD_PAIRS = ATTN_HEADS // 2
D_ATTN = 512
IDX_HEADS = 8
TOPK = 256
POOL_WINDOWS = (2, 4, 8, 16)
POOL_CH = 128
D_POOL = 512
PEER_HEADS = 8
N_KEYS = 128
PEER_HALF = 128
PEER_TOPK = 16

LANES = 128
META_ROWS = 128
ZT_ROWS = 2128
OFF_QT, OFF_KT, OFF_VT, OFF_IQT, OFF_IKT, OFF_IWT = 0, 512, 1024, 1536, 2048, 2112
INT_MIN = -2 ** 31
VMEM_LIMIT = 56 * 1024 * 1024


def _nt(a, b):
    return lax.dot_general(a, b, (((1,), (1,)), ((), ())), preferred_element_type=F32)


def _tn(a, b):
    return lax.dot_general(a, b, (((0,), (0,)), ((), ())), preferred_element_type=F32)


def _mm(a, b):
    return jnp.dot(a, b, preferred_element_type=F32)


def _rms_rows(x, g):
    ms = jnp.mean(x * x, axis=-1, keepdims=True)
    return x * lax.rsqrt(ms + RMS_EPS) * g


def _project_transposed(hn, wt_ref, gq, gk, cos, sin):
    zt = _nt(wt_ref[...], hn)

    def rope(xh):
        x1, x2 = xh[:32], xh[32:]
        return jnp.concatenate([x1 * cos - x2 * sin, x2 * cos + x1 * sin], axis=0)

    def headnorm(xh, g):
        ms = jnp.mean(xh * xh, axis=0, keepdims=True)
        return xh * lax.rsqrt(ms + RMS_EPS) * g

    def head(off, h):
        return zt[off + HEAD_DIM * h: off + HEAD_DIM * (h + 1)]

    q_pairs, k_pairs, v_pairs, qi_pairs = [], [], [], []
    for j in range(HEAD_PAIRS):
        qs = [rope(headnorm(head(OFF_QT, 2 * j + s), gq)) * (HEAD_DIM ** -0.5) for s in range(2)]
        ks = [rope(headnorm(head(OFF_KT, 2 * j + s), gk)) for s in range(2)]
        qis = [rope(head(OFF_IQT, 2 * j + s)) for s in range(2)]
        q_pairs.append(jnp.concatenate(qs, axis=0).T.astype(BF16))
        k_pairs.append(jnp.concatenate(ks, axis=0).T.astype(BF16))
        qi_pairs.append(jnp.concatenate(qis, axis=0).T.astype(BF16))
        v_pairs.append(zt[OFF_VT + LANES * j: OFF_VT + LANES * (j + 1)].T.astype(BF16))
    kit = rope(zt[OFF_IKT: OFF_IKT + HEAD_DIM])
    ki2 = jnp.concatenate([kit, kit], axis=0).T.astype(BF16)
    wi = zt[OFF_IWT: OFF_IWT + IDX_HEADS] * (IDX_HEADS ** -0.5 * HEAD_DIM ** -0.5)
    return q_pairs, k_pairs, v_pairs, qi_pairs, ki2, wi


def _meta_body(x_ref, g1_ref, wt_ref, wpool_ref, gq_ref, gk_ref, cos_ref, sin_ref,
               k_ref, v_ref, ki_ref, u_ref):
    hn = _rms_rows(x_ref[...], g1_ref[...]).astype(BF16)
    _, k_pairs, v_pairs, _, ki2, _ = _project_transposed(
        hn, wt_ref, gq_ref[...], gk_ref[...], cos_ref[...], sin_ref[...])
    for j in range(HEAD_PAIRS):
        k_ref[j] = k_pairs[j]
        v_ref[j] = v_pairs[j]
    ki_ref[...] = ki2
    u_ref[...] = _mm(hn, wpool_ref[...])


def _proj_body(x_ref, g1_ref, wt_ref, wpool_ref, gq_ref, gk_ref, cos_ref, sin_ref,
               poolw_ref, pscale_ref, umeta_ref,
               q_ref, k_ref, v_ref, qi_ref, ki_ref, wi_ref, p_ref, ext_ref, *, tile, tiles_per_seq):
    i = pl.program_id(0)
    hn = _rms_rows(x_ref[...], g1_ref[...]).astype(BF16)
    q_pairs, k_pairs, v_pairs, qi_pairs, ki2, wi = _project_transposed(
        hn, wt_ref, gq_ref[...], gk_ref[...], cos_ref[...], sin_ref[...])
    for j in range(HEAD_PAIRS):
        q_ref[j] = q_pairs[j]
        k_ref[j] = k_pairs[j]
        v_ref[j] = v_pairs[j]
        qi_ref[j] = qi_pairs[j]
    ki_ref[...] = ki2
    wi_ref[...] = wi

    u = _mm(hn, wpool_ref[...])

    @pl.when(i % tiles_per_seq == 0)
    def _():
        ext_ref[0:N_META, :] = umeta_ref[0:N_META, :]

    ext_ref[N_META:N_META + tile, :] = u
    for g, w in enumerate(POOL_WINDOWS):
        cols = slice(POOL_CH * g, POOL_CH * (g + 1))
        acc = ext_ref[N_META:N_META + tile, cols]
        for s in range(1, w):
            acc = acc + ext_ref[N_META - s:N_META - s + tile, cols]
        y = (acc * (1.0 / w) - u[:, cols]).astype(BF16)
        pg = _mm(y, poolw_ref[g]) * pscale_ref[:, cols]
        p_ref[:, cols] = pg.astype(BF16)
    ext_ref[0:N_META, :] = u[tile - N_META:tile, :]


def _sortable(x):
    b = pltpu.bitcast(x, I32)
    k = b ^ ((b >> 31) & 0x7FFFFFFF)
    return jnp.where(k == -1, 0, k)


def _col_reduce(x, op, n_chains=8):
    step = 8 * n_chains
    pair = jnp.maximum if op is jnp.max else jnp.add
    acc = x[0:step]
    for c in range(1, x.shape[0] // step):
        acc = pair(acc, x[c * step:(c + 1) * step])
    return op(acc, axis=0, keepdims=True)


def _attn_block(i, q_ref, qi_ref, wi_ref, k_ref, v_ref, ki_ref, km_ref, vm_ref, kim_ref,
                a_ref, sc_ref, key_ref, idx_ref, bias_ref, *, n_real):
    n_rows = META_ROWS + n_real
    lane_half = lax.broadcasted_iota(I32, (LANES, LANES), 1) >> 6

    def half_mask(pair, h):
        return jnp.where(lane_half == h % 2, pair, jnp.zeros_like(pair))

    sc_ref[0:n_rows, :] = jnp.zeros((n_rows, LANES), F32)

    def add_head_score(h, carry):
        qih = half_mask(qi_ref[h // 2], h)
        w = wi_ref[pl.ds(h, 1), :]
        sc_ref[0:META_ROWS, :] += w * jnp.maximum(_nt(kim_ref[...], qih), 0.0)
        sc_ref[META_ROWS:n_rows, :] += w * jnp.maximum(_nt(ki_ref[0:n_real, :], qih), 0.0)
        return carry

    lax.fori_loop(0, IDX_HEADS, add_head_score, 0)

    row_m = lax.broadcasted_iota(I32, (META_ROWS, LANES), 0)
    row_r = lax.broadcasted_iota(I32, (n_real, LANES), 0)
    qchunk = 2 * i + (lax.broadcasted_iota(I32, (n_real, LANES), 1) >> 6)
    key_ref[0:META_ROWS, :] = jnp.where(row_m < N_META, _sortable(sc_ref[0:META_ROWS, :]), INT_MIN)
    key_ref[META_ROWS:n_rows, :] = jnp.where((row_r >> 6) <= qchunk, _sortable(sc_ref[META_ROWS:n_rows, :]), INT_MIN)

    def count(ref, pred):
        return _col_reduce(pred(ref[0:n_rows, :]).astype(F32), jnp.sum).astype(I32)

    c0 = count(key_ref, lambda x: x >= 0)
    prefix0 = jnp.where(c0 >= TOPK, 0, INT_MIN).astype(I32)

    def search(it, prefix):
        cand = prefix + (jnp.int32(1) << (30 - it))
        c = count(key_ref, lambda x: x >= cand)
        return jnp.where(c >= TOPK, cand, prefix)

    thr = lax.fori_loop(0, 31, search, prefix0)

    thr_eq = jnp.maximum(thr, INT_MIN + 1)
    need = TOPK - count(key_ref, lambda x: x > thr)
    n_eq = count(key_ref, lambda x: x == thr_eq)
    excess = jnp.max(n_eq - need)
    neg = jnp.float32(-jnp.inf)

    @pl.when(excess <= 0)
    def _():
        bias_ref[0:n_rows, :] = jnp.where(key_ref[0:n_rows, :] >= thr_eq, 0.0, neg)

    @pl.when(excess > 0)
    def _():
        rows = lax.broadcasted_iota(I32, (n_rows, LANES), 0)
        idx_ref[0:n_rows, :] = jnp.where(key_ref[0:n_rows, :] == thr_eq, rows, jnp.int32(1 << 20))

        def tie_search(it, p):
            cand = p + (jnp.int32(1) << (11 - it))
            f = count(idx_ref, lambda x: x < cand)
            return jnp.where(f < need, cand, p)

        last_eq = lax.fori_loop(0, 12, tie_search, jnp.zeros((1, LANES), I32))
        bias_ref[0:n_rows, :] = jnp.where(key_ref[0:n_rows, :] > thr, 0.0,
                                          jnp.where(idx_ref[0:n_rows, :] <= last_eq, 0.0, neg))

    def pair_attention(j, carry):
        outs = []
        for s in range(2):
            qh = half_mask(q_ref[j], s)
            lg_m = _nt(km_ref[j], qh) + bias_ref[0:META_ROWS, :]
            lg_r = _nt(k_ref[j, 0:n_real, :], qh) + bias_ref[META_ROWS:n_rows, :]
            mx = jnp.maximum(_col_reduce(lg_m, jnp.max), _col_reduce(lg_r, jnp.max))
            e_m = jnp.exp(lg_m - mx)
            e_r = jnp.exp(lg_r - mx)
            inv = 1.0 / (_col_reduce(e_m, jnp.sum) + _col_reduce(e_r, jnp.sum))
            outs.append(_tn((e_m * inv).astype(BF16), vm_ref[j])
                        + _tn((e_r * inv).astype(BF16), v_ref[j, 0:n_real, :]))
        a_ref[j] = jnp.where(lane_half == 0, outs[0], outs[1]).astype(BF16)
        return carry

    lax.fori_loop(0, HEAD_PAIRS, pair_attention, 0)


def _attn_body(q_ref, qi_ref, wi_ref, k_ref, v_ref, ki_ref, km_ref, vm_ref, kim_ref,
               a_ref, sc_ref, key_ref, idx_ref, bias_ref, *, seq, n_buckets):
    i = pl.program_id(1)
    blocks_per_bucket = seq // LANES // n_buckets
    for b in range(n_buckets):
        @pl.when(i // blocks_per_bucket == b)
        def _(b=b):
            _attn_block(i, q_ref, qi_ref, wi_ref, k_ref, v_ref, ki_ref, km_ref, vm_ref, kim_ref,
                        a_ref, sc_ref, key_ref, idx_ref, bias_ref,
                        n_real=(b + 1) * blocks_per_bucket * LANES)


def _top16_rows(s, with_rank=False):
    neg = jnp.float32(-jnp.inf)
    vals = []
    rank = jnp.full(s.shape, 64.0, F32) if with_rank else None
    for r in range(PEER_TOPK):
        m = jnp.max(s, axis=0, keepdims=True)
        vals.append(m)
        hit = s == m
        if with_rank:
            rank = jnp.where(hit, float(r), rank)
        s = jnp.where(hit, neg, s)
    vals = jnp.concatenate(vals, axis=0)
    return (vals, rank) if with_rank else vals


def _post_body(x_ref, a_ref, p_ref, wout_ref, g2_ref, wq_ref, sk1_ref, sk2_ref,
               h2_ref, hnt_ref, n_ref, e1_ref, r2_ref, e2_ref, *, tile):
    ap = jnp.concatenate([a_ref[j] for j in range(HEAD_PAIRS)] + [p_ref[...]], axis=1)
    h2 = x_ref[...] + _mm(ap, wout_ref[...])
    h2_ref[...] = h2
    hn32 = _rms_rows(h2, g2_ref[...])
    hnt_ref[...] = hn32.T.astype(BF16)
    hn = hn32.astype(BF16)
    q = _mm(hn, wq_ref[...]).astype(BF16)
    neg = jnp.float32(-jnp.inf)
    for h in range(PEER_HEADS):
        base = 2 * PEER_HALF * h
        s1 = _nt(sk1_ref[...], q[:, base:base + PEER_HALF])
        s2 = _nt(sk2_ref[...], q[:, base + PEER_HALF:base + 2 * PEER_HALF])
        v1 = _top16_rows(s1)
        v2, rank2 = _top16_rows(s2, with_rank=True)
        blocks = [v1[0:1] + v2]
        for a in range(1, 8):
            nb = PEER_TOPK // (a + 1)
            rowid = lax.broadcasted_iota(I32, (8, tile), 0)
            blocks.append(jnp.where(rowid < nb, v1[a:a + 1] + v2[0:8], neg))
        blocks.append(v1[8:16] + v2[0:1])
        top = _top16_rows(jnp.concatenate(blocks, axis=0))
        thr = top[PEER_TOPK - 1:PEER_TOPK]
        z = jnp.sum(jnp.exp(top - top[0:1]), axis=0, keepdims=True)
        na = jnp.zeros((PEER_TOPK, tile), F32)
        for b in range(PEER_TOPK):
            na = na + ((v1 + v2[b:b + 1]) >= thr).astype(F32)
        n = jnp.zeros((N_KEYS, tile), F32)
        for a in range(PEER_TOPK):
            n = jnp.where(s1 == v1[a:a + 1], na[a:a + 1], n)
        n_ref[h] = n
        e1_ref[h] = jnp.exp(s1 - v1[0:1]) * (1.0 / z)
        r2_ref[h] = rank2
        e2_ref[h] = jnp.exp(s2 - v2[0:1])


def _gelu(x):
    return 0.5 * x * (1.0 + lax.erf(x * np.float32(2.0 ** -0.5)))


def _peer_body(hnt_ref, u_ref, v_ref, n_ref, e1_ref, r2_ref, e2_ref, h2_ref,
               o_ref, acc_ref, *, e1_per_tile):
    e = pl.program_id(1)
    tile = hnt_ref.shape[1]

    @pl.when(e == 0)
    def _():
        acc_ref[...] = jnp.zeros_like(acc_ref)

    def bf16_rows(row):
        tile16 = jnp.broadcast_to(row, (16, tile)).astype(BF16)
        return jnp.concatenate([tile16] * (N_KEYS // 16), axis=0)

    e2b = [e2_ref[h].astype(BF16) for h in range(PEER_HEADS)]
    r2b = [r2_ref[h].astype(BF16) for h in range(PEER_HEADS)]
    gates = []
    for r in range(e1_per_tile):
        g = None
        for h in range(PEER_HEADS):
            gate = bf16_rows(e1_ref[h, r:r + 1, :]) * e2b[h]
            gh = jnp.where(r2b[h] < bf16_rows(n_ref[h, r:r + 1, :]), gate, jnp.zeros_like(gate))
            g = gh if g is None else g + gh
        gates.append(g)
    at = _mm(u_ref[...], hnt_ref[...])
    parts = [gates[r] * _gelu(at[r * N_KEYS:(r + 1) * N_KEYS]).astype(BF16) for r in range(e1_per_tile)]
    pt = jnp.concatenate(parts, axis=0)
    acc_ref[...] += _tn(pt, v_ref[...])

    @pl.when(e == pl.num_programs(1) - 1)
    def _():
        o_ref[...] = h2_ref[...] + acc_ref[...]


def _rope_tables_t(n_pos):
    inv = ROPE_THETA ** (-jnp.arange(0, HEAD_DIM, 2, dtype=F32) / HEAD_DIM)
    ang = jnp.arange(n_pos, dtype=F32)[:, None] * inv[None, :]
    return jnp.cos(ang).T, jnp.sin(ang).T


def _full(shape):
    return pl.BlockSpec(shape, lambda *_: (0,) * len(shape))


def _params(sem):
    return pltpu.CompilerParams(dimension_semantics=sem, vmem_limit_bytes=VMEM_LIMIT)


def _prepare(meta_tokens, norm1_g, w_in, q_norm_g, k_norm_g, pool_w, pool_scale, w_out, norm2_g,
             peer_wq, sub_keys1, sub_keys2, peer_u, peer_v, seq):
    l = 0
    w = w_in[l]
    wt = jnp.concatenate([w[:, 0:1536], w[:, 2048:2632]], axis=1).T
    wt = jnp.pad(wt, ((0, ZT_ROWS - wt.shape[0]), (0, 0))).astype(BF16)
    cos_t, sin_t = _rope_tables_t(N_META + seq)
    return dict(
        wt=wt,
        wpool=w[:, 1536:2048].astype(BF16),
        g1=norm1_g[l].reshape(1, D_MODEL),
        g2=norm2_g[l].reshape(1, D_MODEL),
        gq=q_norm_g[l].reshape(HEAD_DIM, 1),
        gk=k_norm_g[l].reshape(HEAD_DIM, 1),
        poolw=pool_w[l].astype(BF16),
        pscale=pool_scale[l].reshape(1, D_POOL),
        wout=w_out[l].astype(BF16),
        wq=peer_wq[l].astype(BF16),
        sk1=sub_keys1[l].astype(BF16),
        sk2=sub_keys2[l].astype(BF16),
        ub=peer_u[l].astype(BF16),
        vb=peer_v[l].astype(BF16),
        cos_m=cos_t[:, :META_ROWS], sin_m=sin_t[:, :META_ROWS],
        cos_r=cos_t[:, N_META:], sin_r=sin_t[:, N_META:],
        meta_pad=jnp.pad(meta_tokens.astype(F32), ((0, META_ROWS - N_META), (0, 0))),
    )


def _run_meta_proj(pp):
    D = D_MODEL
    return pl.pallas_call(
        _meta_body,
        grid=(1,),
        in_specs=[_full((META_ROWS, D)), _full((1, D)), _full((ZT_ROWS, D)), _full((D, D_POOL)),
                  _full((HEAD_DIM, 1)), _full((HEAD_DIM, 1)), _full((32, META_ROWS)), _full((32, META_ROWS))],
        out_specs=[_full((HEAD_PAIRS, META_ROWS, LANES)), _full((HEAD_PAIRS, META_ROWS, LANES)),
                   _full((META_ROWS, LANES)), _full((META_ROWS, D_POOL))],
        out_shape=[jax.ShapeDtypeStruct((HEAD_PAIRS, META_ROWS, LANES), BF16),
                   jax.ShapeDtypeStruct((HEAD_PAIRS, META_ROWS, LANES), BF16),
                   jax.ShapeDtypeStruct((META_ROWS, LANES), BF16),
                   jax.ShapeDtypeStruct((META_ROWS, D_POOL), F32)],
        compiler_params=_params(("arbitrary",)),
        name="meta_proj",
    )(pp["meta_pad"], pp["g1"], pp["wt"], pp["wpool"], pp["gq"], pp["gk"], pp["cos_m"], pp["sin_m"])


def _run_proj(x2, pp, umeta, seq):
    N, D = x2.shape
    t1 = 512
    tps = seq // t1
    pair_spec = pl.BlockSpec((HEAD_PAIRS, t1, LANES), lambda i: (0, i, 0))
    pair_shape = jax.ShapeDtypeStruct((HEAD_PAIRS, N, LANES), BF16)
    return pl.pallas_call(
        functools.partial(_proj_body, tile=t1, tiles_per_seq=tps),
        grid=(N // t1,),
        in_specs=[pl.BlockSpec((t1, D), lambda i: (i, 0)), _full((1, D)), _full((ZT_ROWS, D)),
                  _full((D, D_POOL)), _full((HEAD_DIM, 1)), _full((HEAD_DIM, 1)),
                  pl.BlockSpec((32, t1), lambda i: (0, i % tps)), pl.BlockSpec((32, t1), lambda i: (0, i % tps)),
                  _full((4, POOL_CH, POOL_CH)), _full((1, D_POOL)), _full((META_ROWS, D_POOL))],
        out_specs=[pair_spec, pair_spec, pair_spec, pair_spec,
                   pl.BlockSpec((t1, LANES), lambda i: (i, 0)),
                   pl.BlockSpec((IDX_HEADS, t1), lambda i: (0, i)),
                   pl.BlockSpec((t1, D_POOL), lambda i: (i, 0))],
        out_shape=[pair_shape, pair_shape, pair_shape, pair_shape,
                   jax.ShapeDtypeStruct((N, LANES), BF16),
                   jax.ShapeDtypeStruct((IDX_HEADS, N), F32),
                   jax.ShapeDtypeStruct((N, D_POOL), BF16)],
        scratch_shapes=[pltpu.VMEM((N_META + t1, D_POOL), F32)],
        compiler_params=_params(("arbitrary",)),
        name="proj",
    )(x2, pp["g1"], pp["wt"], pp["wpool"], pp["gq"], pp["gk"], pp["cos_r"], pp["sin_r"],
      pp["poolw"], pp["pscale"], umeta)


def _run_attention(q, qi, wi, k, v, ki, km, vm, kim, batch, seq):
    N = batch * seq
    nqb = seq // LANES
    qblk = pl.BlockSpec((HEAD_PAIRS, LANES, LANES), lambda b, i: (0, b * nqb + i, 0))
    kblk = pl.BlockSpec((HEAD_PAIRS, seq, LANES), lambda b, i: (0, b, 0))
    n_rows = META_ROWS + seq
    return pl.pallas_call(
        functools.partial(_attn_body, seq=seq, n_buckets=4),
        grid=(batch, nqb),
        in_specs=[qblk, qblk, pl.BlockSpec((IDX_HEADS, LANES), lambda b, i: (0, b * nqb + i)),
                  kblk, kblk, pl.BlockSpec((seq, LANES), lambda b, i: (b, 0)),
                  _full((HEAD_PAIRS, META_ROWS, LANES)), _full((HEAD_PAIRS, META_ROWS, LANES)),
                  _full((META_ROWS, LANES))],
        out_specs=qblk,
        out_shape=jax.ShapeDtypeStruct((HEAD_PAIRS, N, LANES), BF16),
        scratch_shapes=[pltpu.VMEM((n_rows, LANES), F32), pltpu.VMEM((n_rows, LANES), I32),
                        pltpu.VMEM((n_rows, LANES), I32), pltpu.VMEM((n_rows, LANES), F32)],
        compiler_params=_params(("arbitrary", "arbitrary")),
        name="attention",
    )(q, qi, wi, k, v, ki, km, vm, kim)


def _run_post(x2, a, p, pp):
    N, D = x2.shape
    t3 = 256
    stat_spec = pl.BlockSpec((PEER_HEADS, N_KEYS, t3), lambda i: (0, 0, i))
    return pl.pallas_call(
        functools.partial(_post_body, tile=t3),
        grid=(N // t3,),
        in_specs=[pl.BlockSpec((t3, D), lambda i: (i, 0)), pl.BlockSpec((HEAD_PAIRS, t3, LANES), lambda i: (0, i, 0)),
                  pl.BlockSpec((t3, D_POOL), lambda i: (i, 0)), _full((D, D)), _full((1, D)),
                  _full((D, 2 * PEER_HALF * PEER_HEADS)), _full((N_KEYS, PEER_HALF)), _full((N_KEYS, PEER_HALF))],
        out_specs=[pl.BlockSpec((t3, D), lambda i: (i, 0)), pl.BlockSpec((D, t3), lambda i: (0, i)),
                   stat_spec, stat_spec, stat_spec, stat_spec],
        out_shape=[jax.ShapeDtypeStruct((N, D), F32), jax.ShapeDtypeStruct((D, N), BF16),
                   jax.ShapeDtypeStruct((PEER_HEADS, N_KEYS, N), F32),
                   jax.ShapeDtypeStruct((PEER_HEADS, N_KEYS, N), F32),
                   jax.ShapeDtypeStruct((PEER_HEADS, N_KEYS, N), F32),
                   jax.ShapeDtypeStruct((PEER_HEADS, N_KEYS, N), F32)],
        compiler_params=_params(("arbitrary",)),
        name="post",
    )(x2, a, p, pp["wout"], pp["g2"], pp["wq"], pp["sk1"], pp["sk2"])


def _run_peer(hn2t, n1, e1, r2, e2, h2, pp):
    N, D = h2.shape
    t4 = 512
    e1_per_tile = 8
    et = e1_per_tile * N_KEYS
    n_et = (N_KEYS * N_KEYS) // et
    row_spec = pl.BlockSpec((PEER_HEADS, e1_per_tile, t4), lambda t, e: (0, e, t))
    col_spec = pl.BlockSpec((PEER_HEADS, N_KEYS, t4), lambda t, e: (0, 0, t))
    return pl.pallas_call(
        functools.partial(_peer_body, e1_per_tile=e1_per_tile),
        grid=(N // t4, n_et),
        in_specs=[pl.BlockSpec((D, t4), lambda t, e: (0, t)),
                  pl.BlockSpec((et, D), lambda t, e: (e, 0)), pl.BlockSpec((et, D), lambda t, e: (e, 0)),
                  row_spec, row_spec, col_spec, col_spec,
                  pl.BlockSpec((t4, D), lambda t, e: (t, 0))],
        out_specs=pl.BlockSpec((t4, D), lambda t, e: (t, 0)),
        out_shape=jax.ShapeDtypeStruct((N, D), F32),
        scratch_shapes=[pltpu.VMEM((t4, D), F32)],
        compiler_params=_params(("arbitrary", "arbitrary")),
        name="peer",
    )(hn2t, pp["ub"], pp["vb"], n1, e1, r2, e2, h2)


def kernel(x, meta_tokens, norm1_g, w_in, q_norm_g, k_norm_g, pool_w, pool_scale, w_out, norm2_g,
           peer_wq, sub_keys1, sub_keys2, peer_u, peer_v):
    B, S, D = x.shape
    assert D == D_MODEL and S % 512 == 0 and min(TOPK, S // 4) == TOPK
    x2 = x.reshape(B * S, D)
    pp = _prepare(meta_tokens, norm1_g, w_in, q_norm_g, k_norm_g, pool_w, pool_scale, w_out, norm2_g,
                  peer_wq, sub_keys1, sub_keys2, peer_u, peer_v, S)
    km, vm, kim, umeta = _run_meta_proj(pp)
    q, k, v, qi, ki, wi, p = _run_proj(x2, pp, umeta, S)
    a = _run_attention(q, qi, wi, k, v, ki, km, vm, kim, B, S)
    h2, hn2t, n1, e1, r2, e2 = _run_post(x2, a, p, pp)
    out = _run_peer(hn2t, n1, e1, r2, e2, h2, pp)
    return out.reshape(B, S, D)
```
